```python
import math
import jax, jax.numpy as jnp
from jax import lax
import numpy as np

D_MODEL = 2048
BATCH = 2
SEQ = 4096
DEPTH = 1
DEC_BATCH = 8
DEC_SEQ = 4
PAST_LEN = 16384
PAGE_SIZE = 128

ATT_HEADS = 8
ATT_KV_HEADS = 2
ATT_HEAD_DIM = D_MODEL // 2 // ATT_HEADS
ATT_WIDTH = ATT_HEADS * ATT_HEAD_DIM
KV_WIDTH = ATT_KV_HEADS * ATT_HEAD_DIM
IDX_HEADS = 4
IDX_DIM = 64
IDX_SCALE = (IDX_HEADS ** -0.5) * (IDX_DIM ** -0.5)
TOPK_MAX = 256
Q_BLOCK = 128
MLSTM_HEADS = 4
MLSTM_HEAD_DIM = D_MODEL // 2 // MLSTM_HEADS
MLSTM_WIDTH = MLSTM_HEADS * MLSTM_HEAD_DIM
MLSTM_CHUNK = 64
NUM_BUCKETS = 32
MAX_DISTANCE = 128
MEM_TOKENS = 256
MEM_HEADS = 4
MEM_HEAD_DIM = D_MODEL // MEM_HEADS
D_FF = -(-8 * D_MODEL // (3 * 256)) * 256
ALPHA = (2.0 * DEPTH) ** 0.25
BETA = (8.0 * DEPTH) ** -0.25
LN_EPS = 1e-5

SPLITS = (ATT_WIDTH, KV_WIDTH, KV_WIDTH, IDX_HEADS * IDX_DIM, IDX_HEADS, IDX_DIM,
          MLSTM_WIDTH, MLSTM_WIDTH, MLSTM_WIDTH, MLSTM_WIDTH, MLSTM_HEADS, MLSTM_HEADS)
IN_COLS = sum(SPLITS)
SPLIT_OFFSETS = tuple(int(o) for o in np.cumsum(SPLITS)[:-1])

kernel_name = 'hymba_dsa_mlstm_decoder_step'

F32 = jnp.float32


def layer_norm(x, g, b):
    xf = x.astype(F32)
    mu = xf.mean(-1, keepdims=True)
    var = jnp.square(xf - mu).mean(-1, keepdims=True)
    return ((xf - mu) * lax.rsqrt(var + LN_EPS) * g.astype(F32) + b.astype(F32)).astype(x.dtype)


def rel_bucket(dist):
    exact = NUM_BUCKETS // 2
    d = jnp.maximum(dist, 1).astype(F32)
    large = exact + (jnp.log(d / exact) / math.log(MAX_DISTANCE / exact) * (NUM_BUCKETS - exact)).astype(jnp.int32)
    large = jnp.minimum(large, NUM_BUCKETS - 1)
    return jnp.where(dist < exact, dist, large)


def gather_rows(x, idx):
    return jax.vmap(lambda xb, ib: xb[ib])(x, idx)


def dsa_select(qi, wi, ki, q_pos, topk):
    s = jnp.einsum('bthd,bsd->bths', qi.astype(F32), ki.astype(F32))
    score = jnp.einsum('bths,bth->bts', jax.nn.relu(s), wi.astype(F32)) * IDX_SCALE
    key_pos = jnp.arange(ki.shape[1], dtype=jnp.int32)
    score = jnp.where(key_pos[None, None, :] <= q_pos[None, :, None], score, -jnp.inf)
    _, idx = lax.top_k(score, topk)
    valid = idx <= q_pos[None, :, None]
    return idx, valid


def dsa_attend(q, ksel, vsel, q_pos, idx, valid, rel_bias):
    B, T, H, Dh = q.shape
    KVH = ksel.shape[3]
    G = H // KVH
    K = idx.shape[-1]
    qg = q.reshape(B, T, KVH, G, Dh)
    logits = jnp.einsum('btkgd,btskd->btkgs', qg, ksel).astype(F32) * (Dh ** -0.5)
    dist = jnp.maximum(q_pos[None, :, None] - idx, 0)
    bias = rel_bias[rel_bucket(dist)].astype(F32)
    bias = jnp.moveaxis(bias.reshape(B, T, K, KVH, G), 2, -1)
    logits = jnp.where(valid[:, :, None, None, :], logits + bias, -jnp.inf)
    p = jax.nn.softmax(logits, axis=-1).astype(vsel.dtype)
    out = jnp.einsum('btkgs,btskd->btkgd', p, vsel)
    return out.reshape(B, T, H * Dh)


def prompt_sparse_attention(q, k, v, qi, wi, ki, rel_bias):
    B, T = q.shape[:2]
    qb_len = min(Q_BLOCK, T)
    nb = T // qb_len
    topk = min(TOPK_MAX, T // 4)

    def to_blocks(a):
        return a.reshape((B, nb, qb_len) + a.shape[2:]).swapaxes(0, 1)

    pos = jnp.arange(T, dtype=jnp.int32).reshape(nb, qb_len)

    def one_block(args):
        qb, qib, wib, posb = args
        idx, valid = dsa_select(qib, wib, ki, posb, topk)
        return dsa_attend(qb, gather_rows(k, idx), gather_rows(v, idx), posb, idx, valid, rel_bias)

    out = lax.map(one_block, (to_blocks(q), to_blocks(qi), to_blocks(wi), pos))
    return out.swapaxes(0, 1).reshape(B, T, -1)


def sample_sparse_attention(q, k, v, qi, wi, ki, pool_k, pool_v, pool_ki, page_table, rel_bias):
    B, T = q.shape[:2]
    page = pool_k.shape[1]
    past = page_table.shape[1] * page
    ki_past = pool_ki[page_table].reshape(B, past, pool_ki.shape[-1])
    ki_all = jnp.concatenate([ki_past, ki.astype(ki_past.dtype)], axis=1)
    topk = min(TOPK_MAX, (past + T) // 4)
    q_pos = past + jnp.arange(T, dtype=jnp.int32)
    idx, valid = dsa_select(qi, wi, ki_all, q_pos, topk)
    in_past = (idx < past)[..., None, None]
    pidx = jnp.minimum(idx, past - 1)
    phys = jax.vmap(lambda pt, ii: pt[ii])(page_table, pidx // page)
    off = pidx % page
    nidx = jnp.clip(idx - past, 0, T - 1)
    ksel = jnp.where(in_past, pool_k[phys, off], gather_rows(k, nidx))
    vsel = jnp.where(in_past, pool_v[phys, off], gather_rows(v, nidx))
    return dsa_attend(q, ksel, vsel, q_pos, idx, valid, rel_bias)


def mlstm(q, k, v, i_pre, logf, C0, n0, m0, chunk):
    B, T, H, _ = q.shape
    Dv = v.shape[-1]
    nc = T // chunk

    def to_chunks(a):
        a = a.reshape((B, nc, chunk, H) + a.shape[3:]).astype(F32)
        return jnp.moveaxis(jnp.moveaxis(a, 1, 0), 2, 3)

    causal = jnp.tril(jnp.ones((chunk, chunk), dtype=bool))

    def step(carry, inp):
        C, n, m = carry
        qc, kc, vc, ic, fc = inp
        b = jnp.cumsum(fc, axis=-1)
        dmat = jnp.where(causal, b[..., :, None] - b[..., None, :] + ic[..., None, :], -jnp.inf)
        m_inter = b + m[..., None]
        m_t = jnp.maximum(m_inter, dmat.max(-1))
        w_intra = jnp.exp(dmat - m_t[..., None])
        w_inter = jnp.exp(m_inter - m_t)
        s = jnp.einsum('bhtd,bhsd->bhts', qc, kc) * w_intra
        num = jnp.einsum('bhts,bhse->bhte', s, vc) + w_inter[..., None] * jnp.einsum('bhtd,bhde->bhte', qc, C)
        den = s.sum(-1) + w_inter * jnp.einsum('bhtd,bhd->bht', qc, n)
        h = num / jnp.maximum(jnp.abs(den), jnp.exp(-m_t))[..., None]
        m_new = m_t[..., -1]
        w_state = jnp.exp(b[..., -1:] - b + ic - m_new[..., None])
        decay = jnp.exp(b[..., -1] + m - m_new)
        C_new = decay[..., None, None] * C + jnp.einsum('bhs,bhsd,bhse->bhde', w_state, kc, vc)
        n_new = decay[..., None] * n + jnp.einsum('bhs,bhsd->bhd', w_state, kc)
        return (C_new, n_new, m_new), h

    carry0 = (C0.astype(F32), n0.astype(F32), m0.astype(F32))
    (C, n, m), h = lax.scan(step, carry0, (to_chunks(q), to_chunks(k), to_chunks(v), to_chunks(i_pre), to_chunks(logf)))
    h = jnp.moveaxis(h, 0, 1).swapaxes(2, 3).reshape(B, T, H, Dv)
    return h, (C, n, m)


def memory_cross_attention(x, mem_k, mem_v, w_mq, w_mo):
    B, T, _ = x.shape
    q = (x @ w_mq).reshape(B, T, MEM_HEADS, MEM_HEAD_DIM)
    logits = jnp.einsum('bthd,bmhd->bhtm', q, mem_k).astype(F32) * (MEM_HEAD_DIM ** -0.5)
    p = jax.nn.softmax(logits, axis=-1).astype(mem_v.dtype)
    o = jnp.einsum('bhtm,bmhd->bthd', p, mem_v).reshape(B, T, MEM_HEADS * MEM_HEAD_DIM)
    return o @ w_mo


def trunk_layer(x, attend, C0, n0, m0, chunk, mem_k, mem_v, w_in, b_i, b_f, w_out, ln1_g, ln1_b,
                w_mq, w_mo, ln2_g, ln2_b, w_gate, w_up, w_down, ln3_g, ln3_b):
    B, T, _ = x.shape
    a_q, a_k, a_v, i_q, i_w, i_k, m_q, m_k, m_v, m_o, m_i, m_f = jnp.split(x @ w_in, SPLIT_OFFSETS, axis=-1)
    a_q = a_q.reshape(B, T, ATT_HEADS, ATT_HEAD_DIM)
    a_k = a_k.reshape(B, T, ATT_KV_HEADS, ATT_HEAD_DIM)
    a_v = a_v.reshape(B, T, ATT_KV_HEADS, ATT_HEAD_DIM)
    i_q = i_q.reshape(B, T, IDX_HEADS, IDX_DIM)
    att = attend(a_q, a_k, a_v, i_q, i_w, i_k)

    def heads(a):
        return a.reshape(B, T, MLSTM_HEADS, MLSTM_HEAD_DIM)

    i_pre = m_i.astype(F32) + b_i.astype(F32)
    logf = jax.nn.log_sigmoid(m_f.astype(F32) + b_f.astype(F32))
    h, (C, n, m) = mlstm(heads(m_q), heads(m_k) * (MLSTM_HEAD_DIM ** -0.5), heads(m_v), i_pre, logf, C0, n0, m0, chunk)
    h = jax.nn.sigmoid(m_o) * h.reshape(B, T, MLSTM_WIDTH).astype(x.dtype)

    mix = jnp.concatenate([att, h], axis=-1) @ w_out
    x = layer_norm(ALPHA * x + mix, ln1_g, ln1_b)
    x = layer_norm(ALPHA * x + memory_cross_attention(x, mem_k, mem_v, w_mq, w_mo), ln2_g, ln2_b)
    ffn = (jax.nn.silu(x @ w_gate) * (x @ w_up)) @ w_down
    x = layer_norm(ALPHA * x + ffn, ln3_g, ln3_b)
    return x, (a_k, a_v, i_k), (C, n, m)


def setup_inputs(seed: int = 0) -> dict:
    key = jax.random.key(seed)
    ks = iter(jax.random.split(key, 40))

    def nrm(shape, scale):
        return jax.random.normal(next(ks), shape, F32) * scale

    def gain(shape):
        return 1.0 + nrm(shape, 0.02)

    n_pages = PAST_LEN // PAGE_SIZE
    n_used = DEC_BATCH * n_pages
    n_pool = n_used + max(1, n_used // 4)
    page_table = jax.random.permutation(next(ks), n_pool)[:n_used].reshape(DEC_BATCH, n_pages).astype(jnp.int32)
    d_s = D_MODEL ** -0.5
    return {
        'x_prompt': nrm((BATCH, SEQ, D_MODEL), 1.0),
        'x_sample': nrm((DEC_BATCH, DEC_SEQ, D_MODEL), 1.0),
        'cache_k': nrm((DEPTH, n_pool, PAGE_SIZE, ATT_KV_HEADS, ATT_HEAD_DIM), 1.0),
        'cache_v': nrm((DEPTH, n_pool, PAGE_SIZE, ATT_KV_HEADS, ATT_HEAD_DIM), 1.0),
        'cache_kidx': nrm((DEPTH, n_pool, PAGE_SIZE, IDX_DIM), 1.0),
        'state_C': nrm((DEPTH, DEC_BATCH, MLSTM_HEADS, MLSTM_HEAD_DIM, MLSTM_HEAD_DIM), 0.3),
        'state_n': nrm((DEPTH, DEC_BATCH, MLSTM_HEADS, MLSTM_HEAD_DIM), 1.0),
        'state_m': nrm((DEPTH, DEC_BATCH, MLSTM_HEADS), 1.0),
        'cache_mem_k': nrm((DEPTH, DEC_BATCH, MEM_TOKENS, MEM_HEADS, MEM_HEAD_DIM), 1.0),
        'cache_mem_v': nrm((DEPTH, DEC_BATCH, MEM_TOKENS, MEM_HEADS, MEM_HEAD_DIM), 1.0),
        'page_table': page_table,
        'mem_prompt': nrm((BATCH, MEM_TOKENS, D_MODEL), 1.0),
        'rel_bias': nrm((NUM_BUCKETS, ATT_HEADS), 0.5),
        'w_in': nrm((DEPTH, D_MODEL, IN_COLS), d_s),
        'b_i': nrm((DEPTH, MLSTM_HEADS), 0.1),
        'b_f': jnp.linspace(3.0, 6.0, MLSTM_HEADS, dtype=F32)[None, :] + nrm((DEPTH, MLSTM_HEADS), 0.1),
        'w_out': nrm((DEPTH, D_MODEL, D_MODEL), d_s * BETA),
        'ln1_g': gain((DEPTH, D_MODEL)),
        'ln1_b': nrm((DEPTH, D_MODEL), 0.02),
        'w_mq': nrm((DEPTH, D_MODEL, D_MODEL), d_s),
        'w_mk': nrm((DEPTH, D_MODEL, D_MODEL), d_s),
        'w_mv': nrm((DEPTH, D_MODEL, D_MODEL), d_s),
        'w_mo': nrm((DEPTH, D_MODEL, D_MODEL), d_s * BETA),
        'ln2_g': gain((DEPTH, D_MODEL)),
        'ln2_b': nrm((DEPTH, D_MODEL), 0.02),
        'w_gate': nrm((DEPTH, D_MODEL, D_FF), d_s),
        'w_up': nrm((DEPTH, D_MODEL, D_FF), d_s),
        'w_down': nrm((DEPTH, D_FF, D_MODEL), (D_FF ** -0.5) * BETA),
        'ln3_g': gain((DEPTH, D_MODEL)),
        'ln3_b': nrm((DEPTH, D_MODEL), 0.02),
    }


def reference(x_prompt, x_sample, cache_k, cache_v, cache_kidx, state_C, state_n, state_m, cache_mem_k, cache_mem_v,
              page_table, mem_prompt, rel_bias, w_in, b_i, b_f, w_out, ln1_g, ln1_b, w_mq, w_mk, w_mv, w_mo,
              ln2_g, ln2_b, w_gate, w_up, w_down, ln3_g, ln3_b):
    B, T, _ = x_prompt.shape
    chunk_p = min(MLSTM_CHUNK, T)
    chunk_s = x_sample.shape[1]
    yp, ys = x_prompt, x_sample
    kp, vp, kip, Cp, np_, mp, mkp, mvp = [], [], [], [], [], [], [], []
    ks_, vs_, kis, Cs, ns, ms = [], [], [], [], [], []
    for l in range(DEPTH):
        lw = (w_in[l], b_i[l], b_f[l], w_out[l], ln1_g[l], ln1_b[l], w_mq[l], w_mo[l], ln2_g[l], ln2_b[l],
              w_gate[l], w_up[l], w_down[l], ln3_g[l], ln3_b[l])
        mem_k = (mem_prompt @ w_mk[l]).reshape(B, MEM_TOKENS, MEM_HEADS, MEM_HEAD_DIM)
        mem_v = (mem_prompt @ w_mv[l]).reshape(B, MEM_TOKENS, MEM_HEADS, MEM_HEAD_DIM)
        C0 = jnp.zeros((B, MLSTM_HEADS, MLSTM_HEAD_DIM, MLSTM_HEAD_DIM), F32)
        n0 = jnp.zeros((B, MLSTM_HEADS, MLSTM_HEAD_DIM), F32)
        m0 = jnp.zeros((B, MLSTM_HEADS), F32)

        def prompt_attend(q, k, v, qi, wi, ki):
            return prompt_sparse_attention(q, k, v, qi, wi, ki, rel_bias)

        yp, (k_p, v_p, ki_p), (C_p, n_p, m_p) = trunk_layer(yp, prompt_attend, C0, n0, m0, chunk_p, mem_k, mem_v, *lw)

        pool_k, pool_v, pool_ki = cache_k[l], cache_v[l], cache_kidx[l]

        def sample_attend(q, k, v, qi, wi, ki):
            return sample_sparse_attention(q, k, v, qi, wi, ki, pool_k, pool_v, pool_ki, page_table, rel_bias)

        ys, (k_s, v_s, ki_s), (C_s, n_s, m_s) = trunk_layer(ys, sample_attend, state_C[l], state_n[l], state_m[l], chunk_s,
                                                             cache_mem_k[l], cache_mem_v[l], *lw)
        kp.append(k_p); vp.append(v_p); kip.append(ki_p); Cp.append(C_p); np_.append(n_p); mp.append(m_p)
        mkp.append(mem_k); mvp.append(mem_v)
        ks_.append(k_s); vs_.append(v_s); kis.append(ki_s); Cs.append(C_s); ns.append(n_s); ms.append(m_s)
    new_k_prompt = jnp.stack(kp)
    new_v_prompt = jnp.stack(vp)
    new_kidx_prompt = jnp.stack(kip)
    C_prompt = jnp.stack(Cp)
    n_prompt = jnp.stack(np_)
    m_prompt = jnp.stack(mp)
    mem_k_prompt = jnp.stack(mkp)
    mem_v_prompt = jnp.stack(mvp)
    new_k_sample = jnp.stack(ks_)
    new_v_sample = jnp.stack(vs_)
    new_kidx_sample = jnp.stack(kis)
    C_sample = jnp.stack(Cs)
    n_sample = jnp.stack(ns)
    m_sample = jnp.stack(ms)
    return (yp, ys, new_k_prompt, new_v_prompt, new_kidx_prompt, C_prompt, n_prompt, m_prompt, mem_k_prompt, mem_v_prompt,
            new_k_sample, new_v_sample, new_kidx_sample, C_sample, n_sample, m_sample)
```

```python
import functools
import math

import numpy as np
import jax
import jax.numpy as jnp
from jax import lax
from jax.experimental import pallas as pl
from jax.experimental.pallas import tpu as pltpu

F32 = jnp.float32
BF16 = jnp.bfloat16
I32 = jnp.int32

D_MODEL = 2048
ATT_HEADS = 8
ATT_KV_HEADS = 2
ATT_GROUP = ATT_HEADS // ATT_KV_HEADS
ATT_HEAD_DIM = 128
ATT_WIDTH = ATT_HEADS * ATT_HEAD_DIM
KV_WIDTH = ATT_KV_HEADS * ATT_HEAD_DIM
IDX_HEADS = 4
IDX_DIM = 64
IDX_SCALE = (IDX_HEADS ** -0.5) * (IDX_DIM ** -0.5)
TOPK_MAX = 256
MLSTM_HEADS = 4
MLSTM_HEAD_DIM = 256
MLSTM_WIDTH = MLSTM_HEADS * MLSTM_HEAD_DIM
NUM_BUCKETS = 32
MAX_DISTANCE = 128
MEM_TOKENS = 256
MEM_HEADS = 4
MEM_HEAD_DIM = D_MODEL // MEM_HEADS
D_FF = 5632
DEPTH = 1
ALPHA = (2.0 * DEPTH) ** 0.25
LN_EPS = 1e-5
PAGE_SIZE = 128

LANES = 128
SUBLANES = 8
VMEM_LIMIT = 56 * 1024 * 1024

COL_AQ = 0
COL_AK = COL_AQ + ATT_WIDTH
COL_AV = COL_AK + KV_WIDTH
COL_IQ = COL_AV + KV_WIDTH
COL_MQ = COL_IQ + IDX_HEADS * IDX_DIM
COL_MK = COL_MQ + MLSTM_WIDTH
COL_MV = COL_MK + MLSTM_WIDTH
COL_MO = COL_MV + MLSTM_WIDTH
COL_SMALL = COL_MO + MLSTM_WIDTH
SM_IK = 0
SM_IW = SM_IK + IDX_DIM
SM_MI = SM_IW + IDX_HEADS
SM_MF = SM_MI + MLSTM_HEADS
SM_END = SM_MF + MLSTM_HEADS
PROJ_COLS = 6144

NEG = -1e30
INT_MIN = -2147483648
INT_MAX = 2147483647
MAG_MASK = 0x7FFFFFFF
KEY_NEG_INF = INT_MIN + 0x007FFFFF


def _params(*sem):
    return pltpu.CompilerParams(dimension_semantics=sem, vmem_limit_bytes=VMEM_LIMIT)


def _mm_kernel(x_ref, w_ref, o_ref, xb_ref):
    @pl.when(pl.program_id(1) == 0)
    def _():
        xb_ref[...] = x_ref[...].astype(BF16)

    o_ref[...] = jnp.dot(xb_ref[...], w_ref[...], preferred_element_type=F32).astype(o_ref.dtype)


def _matmul(x, w, *, tm, tn, out_dtype=F32):
    M, K = x.shape
    N = w.shape[1]
    return pl.pallas_call(
        _mm_kernel,
        grid=(M // tm, N // tn),
        in_specs=[pl.BlockSpec((tm, K), lambda i, j: (i, 0)),
                  pl.BlockSpec((K, tn), lambda i, j: (0, j))],
        out_specs=pl.BlockSpec((tm, tn), lambda i, j: (i, j)),
        out_shape=jax.ShapeDtypeStruct((M, N), out_dtype),
        scratch_shapes=[pltpu.VMEM((tm, K), BF16)],
        compiler_params=_params("parallel", "arbitrary"),
        name="matmul",
    )(x, w)


def _layer_norm_rows(y, g, b):
    mu = jnp.mean(y, axis=-1, keepdims=True)
    yc = y - mu
    var = jnp.mean(yc * yc, axis=-1, keepdims=True)
    return yc * lax.rsqrt(var + LN_EPS) * g + b


def _mm_res_ln_kernel(*refs, n_pairs):
    a_refs = refs[:n_pairs]
    w_refs = refs[n_pairs:2 * n_pairs]
    res_ref, g_ref, b_ref, o_ref = refs[2 * n_pairs:]
    acc = None
    for a_ref, w_ref in zip(a_refs, w_refs):
        d = jnp.dot(a_ref[...].astype(BF16), w_ref[...], preferred_element_type=F32)
        acc = d if acc is None else acc + d
    y = ALPHA * res_ref[...] + acc
    o_ref[...] = _layer_norm_rows(y, g_ref[...], b_ref[...])


def _mm_res_ln(a_list, w_list, res, g, b, *, tm):
    M, N = res.shape
    n = len(a_list)
    in_specs = [pl.BlockSpec((tm, a.shape[1]), lambda i: (i, 0)) for a in a_list]
    in_specs += [pl.BlockSpec(w.shape, lambda i: (0, 0), pipeline_mode=pl.Buffered(1)) for w in w_list]
    in_specs += [pl.BlockSpec((tm, N), lambda i: (i, 0)),
                 pl.BlockSpec((1, N), lambda i: (0, 0)),
                 pl.BlockSpec((1, N), lambda i: (0, 0))]
    return pl.pallas_call(
        functools.partial(_mm_res_ln_kernel, n_pairs=n),
        grid=(M // tm,),
        in_specs=in_specs,
        out_specs=pl.BlockSpec((tm, N), lambda i: (i, 0)),
        out_shape=jax.ShapeDtypeStruct((M, N), F32),
        compiler_params=_params("parallel"),
        name="mm_res_ln",
    )(*a_list, *w_list, res, g, b)


def _ffn_kernel(x_ref, wg_ref, wu_ref, wd_ref, g_ref, b_ref, o_ref, xb_ref, acc_ref):
    j = pl.program_id(1)

    @pl.when(j == 0)
    def _():
        xb_ref[...] = x_ref[...].astype(BF16)
        acc_ref[...] = jnp.zeros_like(acc_ref)

    xb = xb_ref[...]
    gate = jnp.dot(xb, wg_ref[...], preferred_element_type=F32)
    up = jnp.dot(xb, wu_ref[...], preferred_element_type=F32)
    mid = (gate * jax.nn.sigmoid(gate)) * up
    acc_ref[...] += jnp.dot(mid.astype(BF16), wd_ref[...], preferred_element_type=F32)

    @pl.when(j == pl.num_programs(1) - 1)
    def _():
        y = ALPHA * x_ref[...] + acc_ref[...]
        o_ref[...] = _layer_norm_rows(y, g_ref[...], b_ref[...])


def _ffn(x, wg, wu, wd, g, b, *, tm, tf):
    M, D = x.shape
    F = wg.shape[1]
    return pl.pallas_call(
        _ffn_kernel,
        grid=(M // tm, F // tf),
        in_specs=[pl.BlockSpec((tm, D), lambda i, j: (i, 0)),
                  pl.BlockSpec((D, tf), lambda i, j: (0, j)),
                  pl.BlockSpec((D, tf), lambda i, j: (0, j)),
                  pl.BlockSpec((tf, D), lambda i, j: (j, 0)),
                  pl.BlockSpec((1, D), lambda i, j: (0, 0)),
                  pl.BlockSpec((1, D), lambda i, j: (0, 0))],
        out_specs=pl.BlockSpec((tm, D), lambda i, j: (i, 0)),
        out_shape=jax.ShapeDtypeStruct((M, D), F32),
        scratch_shapes=[pltpu.VMEM((tm, D), BF16), pltpu.VMEM((tm, D), F32)],
        compiler_params=_params("parallel", "arbitrary"),
        name="ffn",
    )(x, wg, wu, wd, g, b)


def _bias_kernel(relb_ref, dist_ref, o_ref):
    dist = dist_ref[...]
    exact = NUM_BUCKETS // 2
    d = jnp.maximum(dist, 1).astype(F32)
    large = exact + (jnp.log(d / exact) / math.log(MAX_DISTANCE / exact) * (NUM_BUCKETS - exact)).astype(I32)
    large = jnp.minimum(large, NUM_BUCKETS - 1)
    bucket = jnp.where(dist < exact, dist, large)
    for h in range(ATT_HEADS):
        acc = jnp.zeros(dist.shape, F32)
        for bk in range(NUM_BUCKETS):
            acc = jnp.where(bucket == bk, relb_ref[bk, h], acc)
        o_ref[h] = acc


def _bias_tables(rel_bias, dist):
    n = dist.shape[0]
    return pl.pallas_call(
        _bias_kernel,
        in_specs=[pl.BlockSpec(memory_space=pltpu.SMEM),
                  pl.BlockSpec(memory_space=pltpu.VMEM)],
        out_specs=pl.BlockSpec(memory_space=pltpu.VMEM),
        out_shape=jax.ShapeDtypeStruct((ATT_HEADS, n, LANES), F32),
        name="rel_bias_tables",
    )(rel_bias, dist)


def _ukey_to_float(ukey):
    key = ukey ^ INT_MIN
    bits = jnp.where(key >= 0, key, key ^ MAG_MASK)
    return lax.bitcast_convert_type(bits, F32), key


def _kth_largest(count_ge, rows, k):
    def body(t, pu):
        bit = lax.shift_left(jnp.int32(1), 31 - t)
        cand = pu | bit
        thr, key = _ukey_to_float(cand)
        cnt = count_ge(thr)
        accept = jnp.logical_or(cnt >= k, key <= KEY_NEG_INF)
        return jnp.where(accept, cand, pu)

    pu = lax.fori_loop(0, 32, body, jnp.zeros((rows, 1), I32))
    thr, _ = _ukey_to_float(pu)
    return thr


def _tie_limit(count_eq_below, thr, cnt_gt, cnt_ge, k, nbits):
    need = k - cnt_gt

    def body(t, p):
        cand = p | lax.shift_left(jnp.int32(1), nbits - 1 - t)
        return jnp.where(count_eq_below(cand) < need, cand, p)

    p = lax.fori_loop(0, nbits, body, jnp.zeros(thr.shape, I32))
    return p


TQ = 128
TKS = 512
TKA = 128


def _pattn_kernel(q_ref, k_ref, v_ref, iq_ref, smq_ref, smk_ref, bias_ref, o_ref,
                  kb_ref, vb_ref, ikb_ref, score_ref, jlim_ref, m_ref, l_ref, acc_ref, *, topk):
    i = pl.program_id(1)
    kf = float(topk)

    @pl.when(i == 0)
    def _():
        kb_ref[...] = k_ref[...].astype(BF16)
        vb_ref[...] = v_ref[...].astype(BF16)
        lane = lax.broadcasted_iota(I32, smk_ref.shape, 1)
        ikb_ref[...] = jnp.where(lane < IDX_DIM, smk_ref[...], 0.0).astype(BF16)

    iq = iq_ref[...]
    wi = smq_ref[:, SM_IW:SM_IW + IDX_HEADS]
    iq_heads = []
    for p in range(IDX_HEADS // 2):
        blk = iq[:, p * LANES:(p + 1) * LANES]
        iq_heads.append(blk.astype(BF16))
        iq_heads.append(pltpu.roll(blk, IDX_DIM, 1).astype(BF16))
    q_pos = i * TQ + lax.broadcasted_iota(I32, (TQ, TKS), 0)

    def score_body(c, carry):
        kc = ikb_ref[pl.ds(pl.multiple_of(c * TKS, TKS), TKS), :]
        sc = jnp.zeros((TQ, TKS), F32)
        for h in range(IDX_HEADS):
            s = lax.dot_general(iq_heads[h], kc, (((1,), (1,)), ((), ())), preferred_element_type=F32)
            sc = sc + jnp.maximum(s, 0.0) * wi[:, h:h + 1]
        sc = sc * IDX_SCALE
        key_pos = c * TKS + lax.broadcasted_iota(I32, (TQ, TKS), 1)
        sc = jnp.where(key_pos <= q_pos, sc, -jnp.inf)
        for j in range(TKS // LANES):
            score_ref[c * (TKS // LANES) + j] = sc[:, j * LANES:(j + 1) * LANES]
        return carry

    lax.fori_loop(0, (i * TQ + TQ + TKS - 1) // TKS, score_body, 0)

    nch = i + 1

    def count(pred):
        def body(c, acc):
            return acc + jnp.where(pred(score_ref[c], c), 1.0, 0.0)
        acc = lax.fori_loop(0, nch, body, jnp.zeros((TQ, LANES), F32))
        return jnp.sum(acc, axis=1, keepdims=True)

    def count_ge(thr):
        tb = jnp.broadcast_to(thr, (TQ, LANES))
        return count(lambda s, c: s >= tb)

    thr = _kth_largest(count_ge, TQ, kf)
    thr_b = jnp.broadcast_to(thr, (TQ, LANES))
    cnt_gt = count(lambda s, c: s > thr_b)
    cnt_ge = count(lambda s, c: s >= thr_b)
    has_thr = thr > -jnp.inf
    tie = jnp.logical_and(cnt_ge > kf, has_thr)
    jlim_ref[...] = jnp.where(has_thr, INT_MAX, -1).astype(I32)
    lane_idx = lax.broadcasted_iota(I32, (TQ, LANES), 1)

    @pl.when(jnp.max(jnp.where(tie, 1.0, 0.0)) > 0.0)
    def _():
        def count_eq_below(cand):
            cb = jnp.broadcast_to(cand, (TQ, LANES))
            return count(lambda s, c: jnp.logical_and(s == thr_b, c * LANES + lane_idx < cb))
        nbits = max(1, int(math.ceil(math.log2(k_ref.shape[0]))))
        p = _tie_limit(count_eq_below, thr, cnt_gt, cnt_ge, kf, nbits)
        jlim_ref[...] = jnp.where(tie, p, jlim_ref[...])

    jlim_b = jnp.broadcast_to(jlim_ref[...], (TQ, LANES))

    scale = ATT_HEAD_DIM ** -0.5
    qs = []
    for g in range(ATT_KV_HEADS):
        parts = [(q_ref[:, (g * ATT_GROUP + h) * ATT_HEAD_DIM:(g * ATT_GROUP + h + 1) * ATT_HEAD_DIM] * scale).astype(BF16)
                 for h in range(ATT_GROUP)]
        qs.append(jnp.concatenate(parts, axis=0))
    m_ref[...] = jnp.full(m_ref.shape, NEG, F32)
    l_ref[...] = jnp.zeros(l_ref.shape, F32)
    acc_ref[...] = jnp.zeros(acc_ref.shape, F32)

    def attn_body(c, carry):
        s = score_ref[c]
        sel = jnp.logical_or(s > thr_b, jnp.logical_and(s == thr_b, c * LANES + lane_idx <= jlim_b))
        madd = jnp.where(sel, 0.0, NEG)
        madd = jnp.concatenate([madd] * ATT_GROUP, axis=0)
        kind = jnp.minimum(i - c, 2)
        k0 = pl.multiple_of(c * TKA, TKA)
        for g in range(ATT_KV_HEADS):
            kc = kb_ref[pl.ds(k0, TKA), g * ATT_HEAD_DIM:(g + 1) * ATT_HEAD_DIM]
            vc = vb_ref[pl.ds(k0, TKA), g * ATT_HEAD_DIM:(g + 1) * ATT_HEAD_DIM]
            bias = jnp.concatenate([bias_ref[g * ATT_GROUP + h, kind] for h in range(ATT_GROUP)], axis=0)
            logits = lax.dot_general(qs[g], kc, (((1,), (1,)), ((), ())), preferred_element_type=F32)
            logits = logits + bias + madd
            m_old = m_ref[g]
            m_new = jnp.maximum(m_old, jnp.max(logits, axis=1, keepdims=True))
            a = jnp.exp(m_old - m_new)
            p = jnp.exp(logits - m_new)
            l_ref[g] = a * l_ref[g] + jnp.sum(p, axis=1, keepdims=True)
            acc_ref[g] = a * acc_ref[g] + jnp.dot(p.astype(BF16), vc, preferred_element_type=F32)
            m_ref[g] = m_new
        return carry

    lax.fori_loop(0, nch, attn_body, 0)

    for g in range(ATT_KV_HEADS):
        out = acc_ref[g] / l_ref[g]
        for h in range(ATT_GROUP):
            col = (g * ATT_GROUP + h) * ATT_HEAD_DIM
            o_ref[:, col:col + ATT_HEAD_DIM] = out[h * TQ:(h + 1) * TQ].astype(o_ref.dtype)


def _prompt_attention(proj, bias_tab, B, T):
    nq = T // TQ
    topk = min(TOPK_MAX, T // 4)
    return pl.pallas_call(
        functools.partial(_pattn_kernel, topk=topk),
        grid=(B, nq),
        in_specs=[
            pl.BlockSpec((TQ, ATT_WIDTH), lambda b, i: (b * nq + i, COL_AQ // ATT_WIDTH)),
            pl.BlockSpec((T, KV_WIDTH), lambda b, i: (b, COL_AK // KV_WIDTH)),
            pl.BlockSpec((T, KV_WIDTH), lambda b, i: (b, COL_AV // KV_WIDTH)),
            pl.BlockSpec((TQ, IDX_HEADS * IDX_DIM), lambda b, i: (b * nq + i, COL_IQ // (IDX_HEADS * IDX_DIM))),
            pl.BlockSpec((TQ, LANES), lambda b, i: (b * nq + i, COL_SMALL // LANES)),
            pl.BlockSpec((T, LANES), lambda b, i: (b, COL_SMALL // LANES)),
            pl.BlockSpec(bias_tab.shape, lambda b, i: (0, 0, 0, 0)),
        ],
        out_specs=pl.BlockSpec((TQ, ATT_WIDTH), lambda b, i: (b * nq + i, 0)),
        out_shape=jax.ShapeDtypeStruct((B * T, ATT_WIDTH), BF16),
        scratch_shapes=[
            pltpu.VMEM((T, KV_WIDTH), BF16),
            pltpu.VMEM((T, KV_WIDTH), BF16),
            pltpu.VMEM((T, LANES), BF16),
            pltpu.VMEM((T // LANES, TQ, LANES), F32),
            pltpu.VMEM((TQ, 1), I32),
            pltpu.VMEM((ATT_KV_HEADS, ATT_GROUP * TQ, 1), F32),
            pltpu.VMEM((ATT_KV_HEADS, ATT_GROUP * TQ, 1), F32),
            pltpu.VMEM((ATT_KV_HEADS, ATT_GROUP * TQ, ATT_HEAD_DIM), F32),
        ],
        compiler_params=_params("parallel", "arbitrary"),
        name="prompt_sparse_attention",
    )(proj, proj, proj, proj, proj, proj, bias_tab)


TS = 8
SCORE_CHUNK = 2048


def _ssel_kernel(pt_ref, pool_ki_ref, iq_ref, sm_ref, o_ref, kbuf_ref, score_ref, sem, *,
                 n_pages, n_new, topk):
    b = pl.program_id(0)
    past = n_pages * PAGE_SIZE
    total = past + LANES
    kf = float(topk)

    def page_copy(p):
        return pltpu.make_async_copy(pool_ki_ref.at[pt_ref[b, p]],
                                     kbuf_ref.at[pl.ds(pl.multiple_of(p * PAGE_SIZE, PAGE_SIZE), PAGE_SIZE)],
                                     sem.at[0])

    def start_body(p, c):
        page_copy(p).start()
        return c

    def wait_body(p, c):
        page_copy(p).wait()
        return c

    lax.fori_loop(0, n_pages, start_body, 0)
    kbuf_ref[past:past + TS, :] = sm_ref[:, SM_IK:SM_IK + IDX_DIM]
    kbuf_ref[past + TS:total, :] = jnp.zeros((LANES - TS, IDX_DIM), F32)
    lax.fori_loop(0, n_pages, wait_body, 0)

    iq = iq_ref[...]
    wi = sm_ref[:, SM_IW:SM_IW + IDX_HEADS]
    iq_heads = [iq[:, h * IDX_DIM:(h + 1) * IDX_DIM].astype(BF16) for h in range(IDX_HEADS)]
    bounds = list(range(0, total, SCORE_CHUNK)) + [total]
    for c0, c1 in zip(bounds[:-1], bounds[1:]):
        n = c1 - c0
        kc = kbuf_ref[c0:c1, :].astype(BF16)
        sc = jnp.zeros((TS, n), F32)
        for h in range(IDX_HEADS):
            s = lax.dot_general(iq_heads[h], kc, (((1,), (1,)), ((), ())), preferred_element_type=F32)
            sc = sc + jnp.maximum(s, 0.0) * wi[:, h:h + 1]
        sc = sc * IDX_SCALE
        key_pos = c0 + lax.broadcasted_iota(I32, (TS, n), 1)
        q_pos = past + lax.broadcasted_iota(I32, (TS, n), 0)
        valid = jnp.logical_and(key_pos <= q_pos, key_pos < past + n_new)
        score_ref[:, c0:c1] = jnp.where(valid, sc, -jnp.inf)

    score = score_ref[...]

    def count(pred):
        return jnp.sum(jnp.where(pred, 1.0, 0.0), axis=1, keepdims=True)

    thr = _kth_largest(lambda t: count(score >= t), TS, kf)
    cnt_gt = count(score > thr)
    cnt_ge = count(score >= thr)
    has_thr = thr > -jnp.inf
    tie = jnp.logical_and(cnt_ge > kf, has_thr)
    key_idx = lax.broadcasted_iota(I32, (TS, total), 1)
    eq = score == thr
    nbits = int(math.ceil(math.log2(total)))
    p = _tie_limit(lambda cand: count(jnp.logical_and(eq, key_idx < cand)), thr, cnt_gt, cnt_ge, kf, nbits)
    jlim = jnp.where(has_thr, jnp.where(tie, p, INT_MAX), -1)
    sel = jnp.logical_or(score > thr, jnp.logical_and(eq, key_idx <= jlim))
    o_ref[0] = jnp.where(sel, 0.0, NEG)


def _sample_select(page_table, pool_ki, proj_s, B, n_pages, n_new):
    past = n_pages * PAGE_SIZE
    total = past + LANES
    topk = min(TOPK_MAX, (past + n_new) // 4)
    grid_spec = pltpu.PrefetchScalarGridSpec(
        num_scalar_prefetch=1,
        grid=(B,),
        in_specs=[
            pl.BlockSpec(memory_space=pl.ANY),
            pl.BlockSpec((TS, IDX_HEADS * IDX_DIM), lambda b, pt: (b, COL_IQ // (IDX_HEADS * IDX_DIM))),
            pl.BlockSpec((TS, LANES), lambda b, pt: (b, COL_SMALL // LANES)),
        ],
        out_specs=pl.BlockSpec((1, TS, total), lambda b, pt: (b, 0, 0)),
        scratch_shapes=[
            pltpu.VMEM((total, IDX_DIM), F32),
            pltpu.VMEM((TS, total), F32),
            pltpu.SemaphoreType.DMA((1,)),
        ],
    )
    return pl.pallas_call(
        functools.partial(_ssel_kernel, n_pages=n_pages, n_new=n_new, topk=topk),
        grid_spec=grid_spec,
        out_shape=jax.ShapeDtypeStruct((B, TS, total), F32),
        compiler_params=_params("arbitrary"),
        name="sample_select",
    )(page_table, pool_ki, proj_s, proj_s)


PG = 4


def _sattn_kernel(pt_ref, q_ref, *refs, n_pages):
    k_refs = refs[:PG]
    v_refs = refs[PG:2 * PG]
    mpast_ref, mnew_ref, knew_ref, vnew_ref, bias_ref, o_ref, m_ref, l_ref, acc_ref = refs[2 * PG:]
    step = pl.program_id(1)
    n_steps = n_pages // PG
    scale = ATT_HEAD_DIM ** -0.5

    @pl.when(step == 0)
    def _():
        m_ref[...] = jnp.full(m_ref.shape, NEG, F32)
        l_ref[...] = jnp.zeros(l_ref.shape, F32)
        acc_ref[...] = jnp.zeros(acc_ref.shape, F32)

    qs = []
    for g in range(ATT_KV_HEADS):
        parts = [(q_ref[:, (g * ATT_GROUP + h) * ATT_HEAD_DIM:(g * ATT_GROUP + h + 1) * ATT_HEAD_DIM] * scale).astype(BF16)
                 for h in range(ATT_GROUP)]
        qs.append(jnp.concatenate(parts, axis=0))

    def update(g, kc, vc, bias, madd):
        logits = lax.dot_general(qs[g], kc, (((1,), (1,)), ((), ())), preferred_element_type=F32)
        logits = logits + bias + madd
        m_old = m_ref[g]
        m_new = jnp.maximum(m_old, jnp.max(logits, axis=1, keepdims=True))
        a = jnp.exp(m_old - m_new)
        p = jnp.exp(logits - m_new)
        l_ref[g] = a * l_ref[g] + jnp.sum(p, axis=1, keepdims=True)
        acc_ref[g] = a * acc_ref[g] + jnp.dot(p.astype(BF16), vc, preferred_element_type=F32)
        m_ref[g] = m_new

    @pl.when(step < n_steps)
    def _():
        for j in range(PG):
            page = step * PG + j
            kind = jnp.where(page == n_pages - 1, 1, 0)
            madd = mpast_ref[0, :, j * PAGE_SIZE:(j + 1) * PAGE_SIZE]
            madd = jnp.concatenate([madd] * ATT_GROUP, axis=0)
            for g in range(ATT_KV_HEADS):
                kc = k_refs[j][0, :, g * ATT_HEAD_DIM:(g + 1) * ATT_HEAD_DIM].astype(BF16)
                vc = v_refs[j][0, :, g * ATT_HEAD_DIM:(g + 1) * ATT_HEAD_DIM].astype(BF16)
                bias = jnp.concatenate([bias_ref[g * ATT_GROUP + h, kind] for h in range(ATT_GROUP)], axis=0)
                update(g, kc, vc, bias, madd)

    @pl.when(step == n_steps)
    def _():
        madd = jnp.concatenate([mnew_ref[0]] * ATT_GROUP, axis=0)
        pad = jnp.zeros((LANES - TS, ATT_HEAD_DIM), F32)
        for g in range(ATT_KV_HEADS):
            kc = jnp.concatenate([knew_ref[:, g * ATT_HEAD_DIM:(g + 1) * ATT_HEAD_DIM], pad], axis=0).astype(BF16)
            vc = jnp.concatenate([vnew_ref[:, g * ATT_HEAD_DIM:(g + 1) * ATT_HEAD_DIM], pad], axis=0).astype(BF16)
            bias = jnp.concatenate([bias_ref[g * ATT_GROUP + h, 2] for h in range(ATT_GROUP)], axis=0)
            update(g, kc, vc, bias, madd)
            out = acc_ref[g] / l_ref[g]
            for h in range(ATT_GROUP):
                col = (g * ATT_GROUP + h) * ATT_HEAD_DIM
                o_ref[:, col:col + ATT_HEAD_DIM] = out[h * TS:(h + 1) * TS]


def _sample_attention(page_table, pool_k, pool_v, proj_s, madd, bias_tab, B, n_pages):
    n_steps = n_pages // PG
    past_blocks = n_pages // PG

    def page_spec(j):
        return pl.BlockSpec((1, PAGE_SIZE, KV_WIDTH),
                            lambda b, s, pt: (pt[b, jnp.minimum(s * PG + j, n_pages - 1)], 0, 0))

    grid_spec = pltpu.PrefetchScalarGridSpec(
        num_scalar_prefetch=1,
        grid=(B, n_steps + 1),
        in_specs=(
            [pl.BlockSpec((TS, ATT_WIDTH), lambda b, s, pt: (b, COL_AQ // ATT_WIDTH))]
            + [page_spec(j) for j in range(PG)]
            + [page_spec(j) for j in range(PG)]
            + [pl.BlockSpec((1, TS, PG * PAGE_SIZE), lambda b, s, pt: (b, 0, jnp.minimum(s, past_blocks - 1))),
               pl.BlockSpec((1, TS, LANES), lambda b, s, pt: (b, 0, n_pages)),
               pl.BlockSpec((TS, KV_WIDTH), lambda b, s, pt: (b, COL_AK // KV_WIDTH)),
               pl.BlockSpec((TS, KV_WIDTH), lambda b, s, pt: (b, COL_AV // KV_WIDTH)),
               pl.BlockSpec(bias_tab.shape, lambda b, s, pt: (0, 0, 0, 0))]),
        out_specs=pl.BlockSpec((TS, ATT_WIDTH), lambda b, s, pt: (b, 0)),
        scratch_shapes=[
            pltpu.VMEM((ATT_KV_HEADS, ATT_GROUP * TS, 1), F32),
            pltpu.VMEM((ATT_KV_HEADS, ATT_GROUP * TS, 1), F32),
            pltpu.VMEM((ATT_KV_HEADS, ATT_GROUP * TS, ATT_HEAD_DIM), F32),
        ],
    )
    return pl.pallas_call(
        functools.partial(_sattn_kernel, n_pages=n_pages),
        grid_spec=grid_spec,
        out_shape=jax.ShapeDtypeStruct((B * TS, ATT_WIDTH), F32),
        compiler_params=_params("parallel", "arbitrary"),
        name="sample_sparse_attention",
    )(page_table, proj_s, *([pool_k] * PG), *([pool_v] * PG), madd, madd, proj_s, proj_s, bias_tab)


def _mlstm_kernel(bi_ref, bf_ref, q_ref, k_ref, v_ref, og_ref, gate_ref, c0_ref, n0_ref, m0_ref,
                  h_ref, c_ref, n_ref, m_ref, *, L, rows, nvalid):
    hd = pl.program_id(1)
    ci = pl.program_id(2)

    @pl.when(ci == 0)
    def _():
        c_ref[...] = c0_ref[...]
        n_ref[...] = n0_ref[...]
        m_ref[...] = m0_ref[...]

    def padded(ref):
        x = ref[...]
        if rows < L:
            x = jnp.concatenate([x, jnp.zeros((L - rows, x.shape[1]), x.dtype)], axis=0)
        return x

    q = padded(q_ref)
    k = padded(k_ref) * (MLSTM_HEAD_DIM ** -0.5)
    v = padded(v_ref)
    C = c_ref[0, 0]
    n_row = n_ref[0, 0]
    m_prev = m_ref[0, 0]

    pos_row = lax.broadcasted_iota(I32, (1, L), 1)
    i_row = gate_ref[0, 0, 0:1, :] + bi_ref[hd]
    f_pre = gate_ref[0, 0, 1:2, :] + bf_ref[hd]
    logf_row = jnp.minimum(f_pre, 0.0) - jnp.log1p(jnp.exp(-jnp.abs(f_pre)))
    if nvalid < L:
        i_row = jnp.where(pos_row < nvalid, i_row, NEG)
        logf_row = jnp.where(pos_row < nvalid, logf_row, 0.0)

    r_idx = lax.broadcasted_iota(I32, (L, L), 0)
    c_idx = lax.broadcasted_iota(I32, (L, L), 1)
    causal = c_idx <= r_idx
    eye = c_idx == r_idx
    b_col = jnp.sum(jnp.where(causal, jnp.broadcast_to(logf_row, (L, L)), 0.0), axis=1, keepdims=True)
    i_col = jnp.sum(jnp.where(eye, jnp.broadcast_to(i_row, (L, L)), 0.0), axis=1, keepdims=True)
    g_col = i_col - b_col
    g_row = jnp.sum(jnp.where(eye, jnp.broadcast_to(g_col, (L, L)), 0.0), axis=0, keepdims=True)

    dmat = jnp.where(causal, b_col + g_row, NEG)
    m_inter = b_col + m_prev
    m_t = jnp.maximum(m_inter, jnp.max(dmat, axis=1, keepdims=True))
    w_intra = jnp.where(causal, jnp.exp(dmat - m_t), 0.0)
    w_inter = jnp.exp(m_inter - m_t)

    qb = q.astype(BF16)
    kb = k.astype(BF16)
    vb = v.astype(BF16)
    s = lax.dot_general(qb, kb, (((1,), (1,)), ((), ())), preferred_element_type=F32) * w_intra
    num = jnp.dot(s.astype(BF16), vb, preferred_element_type=F32) \
        + w_inter * jnp.dot(qb, C.astype(BF16), preferred_element_type=F32)
    den = jnp.sum(s, axis=1, keepdims=True) + w_inter * jnp.sum(q * n_row, axis=1, keepdims=True)
    h = num / jnp.maximum(jnp.abs(den), jnp.exp(-m_t))
    og = og_ref[...]
    h_ref[...] = (jax.nn.sigmoid(og) * h[:rows]).astype(h_ref.dtype)

    m_new = m_t[L - 1:L, :]
    b_last = b_col[L - 1:L, :]
    w_state = jnp.exp(b_last + g_col - m_new)
    decay = jnp.exp(b_last + m_prev - m_new)
    kw = k * w_state
    c_ref[0, 0] = decay * C + lax.dot_general(kw.astype(BF16), vb, (((0,), (0,)), ((), ())),
                                              preferred_element_type=F32)
    n_ref[0, 0] = decay * n_row + jnp.sum(kw, axis=0, keepdims=True)
    m_ref[0, 0] = m_new


def _mlstm(proj, gates, b_i, b_f, C0, n0, m0, *, B, T_rows, L, rows, nvalid, out_dtype):
    H = MLSTM_HEADS
    Dh = MLSTM_HEAD_DIM
    nc = T_rows // rows

    def col_spec(col0):
        return pl.BlockSpec((rows, Dh), lambda b, h, c: (b * nc + c, col0 // Dh + h))

    smem = pl.BlockSpec(memory_space=pltpu.SMEM)
    state_specs = [pl.BlockSpec((1, 1, Dh, Dh), lambda b, h, c: (b, h, 0, 0)),
                   pl.BlockSpec((1, 1, 1, Dh), lambda b, h, c: (b, h, 0, 0)),
                   pl.BlockSpec((1, 1, 1, 1), lambda b, h, c: (b, h, 0, 0))]
    return pl.pallas_call(
        functools.partial(_mlstm_kernel, L=L, rows=rows, nvalid=nvalid),
        grid=(B, H, nc),
        in_specs=[smem, smem, col_spec(COL_MQ), col_spec(COL_MK), col_spec(COL_MV), col_spec(COL_MO),
                  pl.BlockSpec((1, 1, 2, L), lambda b, h, c: (b, h, 0, c))] + state_specs,
        out_specs=[pl.BlockSpec((rows, Dh), lambda b, h, c: (b * nc + c, h))] + state_specs,
        out_shape=[jax.ShapeDtypeStruct((B * T_rows, MLSTM_WIDTH), out_dtype),
                   jax.ShapeDtypeStruct((B, H, Dh, Dh), F32),
                   jax.ShapeDtypeStruct((B, H, 1, Dh), F32),
                   jax.ShapeDtypeStruct((B, H, 1, 1), F32)],
        compiler_params=_params("parallel", "parallel", "arbitrary"),
        name="mlstm",
    )(b_i, b_f, proj, proj, proj, proj, gates, C0, n0, m0)


def _xattn_kernel(q_ref, k_ref, v_ref, o_ref):
    scale = MEM_HEAD_DIM ** -0.5
    for h in range(MEM_HEADS):
        sl = slice(h * MEM_HEAD_DIM, (h + 1) * MEM_HEAD_DIM)
        qh = q_ref[:, sl].astype(BF16)
        kh = k_ref[0, :, sl].astype(BF16)
        vh = v_ref[0, :, sl].astype(BF16)
        logits = lax.dot_general(qh, kh, (((1,), (1,)), ((), ())), preferred_element_type=F32) * scale
        mx = jnp.max(logits, axis=1, keepdims=True)
        p = jnp.exp(logits - mx)
        p = p / jnp.sum(p, axis=1, keepdims=True)
        o_ref[:, sl] = jnp.dot(p.astype(BF16), vh, preferred_element_type=F32).astype(o_ref.dtype)


def _xattn(qm, mem_k, mem_v, *, B, tm, out_dtype):
    M, D = qm.shape
    nt = M // B // tm
    return pl.pallas_call(
        _xattn_kernel,
        grid=(B, nt),
        in_specs=[pl.BlockSpec((tm, D), lambda b, i: (b * nt + i, 0)),
                  pl.BlockSpec((1, MEM_TOKENS, D), lambda b, i: (b, 0, 0)),
                  pl.BlockSpec((1, MEM_TOKENS, D), lambda b, i: (b, 0, 0))],
        out_specs=pl.BlockSpec((tm, D), lambda b, i: (b * nt + i, 0)),
        out_shape=jax.ShapeDtypeStruct((M, D), out_dtype),
        compiler_params=_params("parallel", "arbitrary"),
        name="memory_cross_attention",
    )(qm, mem_k, mem_v)


def _bias_distances():
    r = np.arange(TQ)[:, None]
    c = np.arange(LANES)[None, :]
    far = np.full((TQ, LANES), 4 * MAX_DISTANCE, np.int32)
    prompt = [np.maximum(r - c, 0), TQ + r - c, far]
    rs = np.arange(TS)[:, None]
    sample = [far[:TS], PAGE_SIZE + rs - c, np.maximum(rs - c, 0)]
    return np.concatenate(prompt + sample, axis=0).astype(np.int32)


def _gate_rows(proj, B, T, L):
    g = proj[:, COL_SMALL + SM_MI:COL_SMALL + SM_END].reshape(B, T, 2, MLSTM_HEADS)
    g = jnp.transpose(g, (0, 3, 2, 1))
    if T % L:
        g = jnp.pad(g, ((0, 0), (0, 0), (0, 0), (0, L - T % L)))
    return g


def _post_mixer(x, att, hm, mem_k, mem_v, w, *, B, tm, tm_x, tm_f, mid_dtype):
    x1 = _mm_res_ln([att, hm], [w["out_a"], w["out_m"]], x, w["ln1_g"], w["ln1_b"], tm=tm)
    qm = _matmul(x1, w["mq"], tm=tm, tn=D_MODEL // 2, out_dtype=mid_dtype)
    o = _xattn(qm, mem_k, mem_v, B=B, tm=tm_x, out_dtype=mid_dtype)
    x2 = _mm_res_ln([o], [w["mo"]], x1, w["ln2_g"], w["ln2_b"], tm=tm)
    return _ffn(x2, w["gate"], w["up"], w["down"], w["ln3_g"], w["ln3_b"], tm=tm_f, tf=512)


def kernel(x_prompt, x_sample, cache_k, cache_v, cache_kidx, state_C, state_n, state_m, cache_mem_k, cache_mem_v,
           page_table, mem_prompt, rel_bias, w_in, b_i, b_f, w_out, ln1_g, ln1_b, w_mq, w_mk, w_mv, w_mo,
           ln2_g, ln2_b, w_gate, w_up, w_down, ln3_g, ln3_b):
    B, T, D = x_prompt.shape
    SB, ST, _ = x_sample.shape
    n_pages = page_table.shape[1]
    H, Dh = MLSTM_HEADS, MLSTM_HEAD_DIM

    wi = w_in[0]
    offs = np.cumsum((ATT_WIDTH, KV_WIDTH, KV_WIDTH, IDX_HEADS * IDX_DIM, IDX_HEADS, IDX_DIM,
                      MLSTM_WIDTH, MLSTM_WIDTH, MLSTM_WIDTH, MLSTM_WIDTH, MLSTM_HEADS, MLSTM_HEADS))[:-1]
    a_q, a_k, a_v, i_q, i_w, i_k, m_q, m_k, m_v, m_o, m_i, m_f = jnp.split(wi, [int(o) for o in offs], axis=1)
    w_cat = jnp.concatenate([a_q, a_k, a_v, i_q, m_q, m_k, m_v, m_o, i_k, i_w, m_i, m_f,
                             jnp.zeros((D, PROJ_COLS - COL_SMALL - SM_END), F32)], axis=1).astype(BF16)
    w = {
        "out_a": w_out[0, :ATT_WIDTH].astype(BF16), "out_m": w_out[0, ATT_WIDTH:].astype(BF16),
        "mq": w_mq[0].astype(BF16), "mo": w_mo[0].astype(BF16),
        "gate": w_gate[0].astype(BF16), "up": w_up[0].astype(BF16), "down": w_down[0].astype(BF16),
        "ln1_g": ln1_g, "ln1_b": ln1_b, "ln2_g": ln2_g, "ln2_b": ln2_b, "ln3_g": ln3_g, "ln3_b": ln3_b,
    }

    bias_all = _bias_tables(rel_bias, jnp.asarray(_bias_distances()))
    bias_p = bias_all[:, :3 * TQ].reshape(ATT_HEADS, 3, TQ, LANES)
    bias_s = bias_all[:, 3 * TQ:].reshape(ATT_HEADS, 3, TS, LANES)

    xp = x_prompt.reshape(B * T, D)
    proj = _matmul(xp, w_cat, tm=1024, tn=512)
    mem_p = mem_prompt.reshape(B * MEM_TOKENS, D)
    mem_k = _matmul(mem_p, w_mk[0].astype(BF16), tm=B * MEM_TOKENS, tn=512)
    mem_v = _matmul(mem_p, w_mv[0].astype(BF16), tm=B * MEM_TOKENS, tn=512)

    att = _prompt_attention(proj, bias_p, B, T)
    LP = 256
    hm, C_p, n_p, m_p = _mlstm(proj, _gate_rows(proj, B, T, LP), b_i[0], b_f[0],
                               jnp.zeros((B, H, Dh, Dh), F32), jnp.zeros((B, H, 1, Dh), F32),
                               jnp.zeros((B, H, 1, 1), F32),
                               B=B, T_rows=T, L=LP, rows=LP, nvalid=LP, out_dtype=BF16)
    yp = _post_mixer(xp, att, hm, mem_k.reshape(B, MEM_TOKENS, D), mem_v.reshape(B, MEM_TOKENS, D), w,
                     B=B, tm=256, tm_x=512, tm_f=512, mid_dtype=BF16)

    xs = jnp.pad(x_sample, ((0, 0), (0, TS - ST), (0, 0))).reshape(SB * TS, D)
    proj_s = _matmul(xs, w_cat, tm=SB * TS, tn=512)
    pool_k = cache_k[0].reshape(-1, PAGE_SIZE, KV_WIDTH)
    pool_v = cache_v[0].reshape(-1, PAGE_SIZE, KV_WIDTH)
    madd = _sample_select(page_table, cache_kidx[0], proj_s, SB, n_pages, ST)
    att_s = _sample_attention(page_table, pool_k, pool_v, proj_s, madd, bias_s, SB, n_pages)
    LS = 128
    hm_s, C_s, n_s, m_s = _mlstm(proj_s, _gate_rows(proj_s, SB, TS, LS), b_i[0], b_f[0],
                                 state_C[0], state_n[0].reshape(SB, H, 1, Dh), state_m[0].reshape(SB, H, 1, 1),
                                 B=SB, T_rows=TS, L=LS, rows=TS, nvalid=ST, out_dtype=F32)
    ys = _post_mixer(xs, att_s, hm_s, cache_mem_k[0].reshape(SB, MEM_TOKENS, D),
                     cache_mem_v[0].reshape(SB, MEM_TOKENS, D), w,
                     B=SB, tm=SB * TS, tm_x=TS, tm_f=SB * TS, mid_dtype=F32)

    def rows_s(a, width):
        return a.reshape(SB, TS, width)[:, :ST]

    return (
        yp.reshape(B, T, D),
        rows_s(ys, D),
        proj[:, COL_AK:COL_AK + KV_WIDTH].reshape(1, B, T, ATT_KV_HEADS, ATT_HEAD_DIM),
        proj[:, COL_AV:COL_AV + KV_WIDTH].reshape(1, B, T, ATT_KV_HEADS, ATT_HEAD_DIM),
        proj[:, COL_SMALL + SM_IK:COL_SMALL + SM_IK + IDX_DIM].reshape(1, B, T, IDX_DIM),
        C_p[None],
        n_p.reshape(1, B, H, Dh),
        m_p.reshape(1, B, H),
        mem_k.reshape(1, B, MEM_TOKENS, MEM_HEADS, MEM_HEAD_DIM),
        mem_v.reshape(1, B, MEM_TOKENS, MEM_HEADS, MEM_HEAD_DIM),
        rows_s(proj_s[:, COL_AK:COL_AK + KV_WIDTH], KV_WIDTH).reshape(1, SB, ST, ATT_KV_HEADS, ATT_HEAD_DIM),
        rows_s(proj_s[:, COL_AV:COL_AV + KV_WIDTH], KV_WIDTH).reshape(1, SB, ST, ATT_KV_HEADS, ATT_HEAD_DIM),
        rows_s(proj_s[:, COL_SMALL + SM_IK:COL_SMALL + SM_IK + IDX_DIM], IDX_DIM).reshape(1, SB, ST, IDX_DIM),
        C_s[None],
        n_s.reshape(1, SB, H, Dh),
        m_s.reshape(1, SB, H),
    )
```
